```python
import math
import jax, jax.numpy as jnp
from jax import lax
import numpy as np

D_MODEL = 2048
BATCH = 4
SEQ = 2048
DEPTH = 2
DEC_BATCH = 8
DEC_SEQ = 1
PAST_LEN = 16384
PAGE_SIZE = 128

N_HEADS = 16
HEAD_DIM = D_MODEL // N_HEADS
DIFF_HALF = HEAD_DIM // 2
D_FF = 4 * D_MODEL
PLE_DIM = 256
MOBA_BLOCK = 256
MOBA_TOPK = 3
MOBA_Q_CHUNK = 16
Q_BLOCK = 128
N_DIFF_LAYERS = (DEPTH + 1) // 2
NORM_EPS = 1e-6
SUBLN_EPS = 1e-5

kernel_name = 'hybrid_diffattn_moba_decoder_step'


def rmsnorm(x, g, eps=NORM_EPS):
    xf = x.astype(jnp.float32)
    y = xf * lax.rsqrt(jnp.mean(xf * xf, axis=-1, keepdims=True) + eps)
    return (y * g.astype(jnp.float32)).astype(x.dtype)


def alibi_slopes():
    return 2.0 ** (-8.0 * jnp.arange(1, N_HEADS + 1, dtype=jnp.float32) / N_HEADS)


def lambda_of(lam_p, lam_init):
    lp = lam_p.astype(jnp.float32)
    return jnp.exp(jnp.sum(lp[0] * lp[1])) - jnp.exp(jnp.sum(lp[2] * lp[3])) + lam_init


def map_query_blocks(fn, q, q_pos):
    B, T = q.shape[:2]
    if T <= Q_BLOCK or T % Q_BLOCK:
        return fn(q, q_pos)
    nb = T // Q_BLOCK
    qb = jnp.moveaxis(q.reshape(B, nb, Q_BLOCK, *q.shape[2:]), 1, 0)
    out = lax.map(lambda a: fn(a[0], a[1]), (qb, q_pos.reshape(nb, Q_BLOCK)))
    return jnp.moveaxis(out, 0, 1).reshape(B, T, *out.shape[3:])


def diff_block(q, k_all, v_all, q_pos, lam):
    B, L = k_all.shape[:2]
    k = k_all.reshape(B, L, N_HEADS, 2, DIFF_HALF).astype(jnp.float32)
    dist = q_pos[:, None] - jnp.arange(L, dtype=jnp.int32)[None, :]
    bias = -alibi_slopes()[:, None, None] * dist.astype(jnp.float32)[None]
    s = jnp.einsum('bqhcd,bkhcd->bchqk', q.astype(jnp.float32), k) * (DIFF_HALF ** -0.5) + bias
    s = jnp.where(dist >= 0, s, -jnp.inf)
    p = jax.nn.softmax(s, axis=-1)
    a = p[:, 0] - lam * p[:, 1]
    return jnp.einsum('bhqk,bkhd->bqhd', a, v_all.astype(jnp.float32))


def diff_mixer(q, k_all, v_all, q_pos, lam_p, subln_g, lam_init):
    B, T = q.shape[:2]
    lam = lambda_of(lam_p, lam_init)
    qs = q.reshape(B, T, N_HEADS, 2, DIFF_HALF)
    o = map_query_blocks(lambda qb, pb: diff_block(qb, k_all, v_all, pb, lam), qs, q_pos)
    o = rmsnorm(o, subln_g, SUBLN_EPS) * (1.0 - lam_init)
    return o.astype(q.dtype)


def moba_mixer(q, k_all, v_all, q_pos):
    B, T = q.shape[:2]
    L = k_all.shape[1]
    f32 = jnp.float32
    slopes = alibi_slopes()
    nb = -(-L // MOBA_BLOCK)
    n_full = L // MOBA_BLOCK
    k_sel = min(MOBA_TOPK, n_full)

    def to_blocks(a):
        a = jnp.pad(a, ((0, 0), (0, nb * MOBA_BLOCK - L), (0, 0), (0, 0)))
        return a.reshape(B, nb, MOBA_BLOCK, N_HEADS, HEAD_DIM).transpose(0, 3, 1, 2, 4)

    kb, vb = to_blocks(k_all), to_blocks(v_all)
    own = q_pos // MOBA_BLOCK
    own_idx = jnp.broadcast_to(own[None, :, None, None], (B, T, N_HEADS, 1)).astype(jnp.int32)
    if k_sel > 0:
        k_mean = jnp.mean(kb[:, :, :n_full].astype(f32), axis=3)
        gate = jnp.einsum('bthd,bhnd->bthn', q.astype(f32), k_mean)
        past = jnp.arange(n_full)[None, :] < own[:, None]
        gate = jnp.where(past[None, :, None, :], gate, -jnp.inf)
        _, top = lax.top_k(gate, k_sel)
        idx = jnp.concatenate([top.astype(jnp.int32), own_idx], axis=-1)
        valid = jnp.concatenate([jnp.arange(k_sel)[None, :] < own[:, None],
                                 jnp.ones((T, 1), dtype=bool)], axis=-1)
    else:
        idx = own_idx
        valid = jnp.ones((T, 1), dtype=bool)
    S = idx.shape[-1]
    qc_len = MOBA_Q_CHUNK if T % MOBA_Q_CHUNK == 0 else T
    nc = T // qc_len
    blk = jnp.arange(MOBA_BLOCK, dtype=jnp.int32)
    heads = jnp.arange(N_HEADS)[None, :, None]
    scale = HEAD_DIM ** -0.5

    def chunk(args):
        qc, ic, pc, vc, bi = args
        kg = kb[bi, heads, ic].astype(f32)
        vg = vb[bi, heads, ic].astype(f32)
        kpos = ic[..., None] * MOBA_BLOCK + blk
        dist = pc[:, None, None, None] - kpos
        s = jnp.einsum('qhd,qhsnd->qhsn', qc.astype(f32), kg) * scale \
            - slopes[None, :, None, None] * dist.astype(f32)
        s = jnp.where(vc[:, None, :, None] & (dist >= 0), s, -jnp.inf)
        pr = jax.nn.softmax(s.reshape(s.shape[0], s.shape[1], -1), axis=-1).reshape(s.shape)
        return jnp.einsum('qhsn,qhsnd->qhd', pr, vg)

    out = lax.map(chunk, (q.reshape(B * nc, qc_len, N_HEADS, HEAD_DIM),
                          idx.reshape(B * nc, qc_len, N_HEADS, S),
                          jnp.tile(q_pos.reshape(nc, qc_len), (B, 1)),
                          jnp.tile(valid.reshape(nc, qc_len, S), (B, 1, 1)),
                          jnp.repeat(jnp.arange(B, dtype=jnp.int32), nc)))
    return out.reshape(B, T, N_HEADS, HEAD_DIM).astype(q.dtype)


def run_group(x, p, weights, cache_k, cache_v, page_table):
    (g_mix, w_qkv, w_o, diff_lambda, diff_subln, g_ffn, w_up, w_down,
     g_ple, w_ple_gate, w_ple_proj, g_final) = weights
    B, T, _ = x.shape
    h = x
    ks, vs = [], []
    for i in range(DEPTH):
        hn = rmsnorm(h, g_mix[i])
        q, k, v = [a.reshape(B, T, N_HEADS, HEAD_DIM) for a in jnp.split(hn @ w_qkv[i], 3, axis=-1)]
        ks.append(k)
        vs.append(v)
        if cache_k is None:
            k_all, v_all = k, v
        else:
            kp = cache_k[i, page_table].reshape(B, -1, N_HEADS, HEAD_DIM).astype(k.dtype)
            vp = cache_v[i, page_table].reshape(B, -1, N_HEADS, HEAD_DIM).astype(v.dtype)
            k_all = jnp.concatenate([kp, k], axis=1)
            v_all = jnp.concatenate([vp, v], axis=1)
        L = k_all.shape[1]
        q_pos = jnp.arange(L - T, L, dtype=jnp.int32)
        if i % 2 == 0:
            lam_init = 0.8 - 0.6 * math.exp(-0.3 * i)
            o = diff_mixer(q, k_all, v_all, q_pos, diff_lambda[i // 2], diff_subln[i // 2], lam_init)
        else:
            o = moba_mixer(q, k_all, v_all, q_pos)
        h = h + o.reshape(B, T, D_MODEL) @ w_o[i]
        u = rmsnorm(h, g_ffn[i]) @ w_up[i]
        h = h + jnp.square(jax.nn.relu(u)) @ w_down[i]
        gate = jax.nn.sigmoid((rmsnorm(h, g_ple[i]) @ w_ple_gate[i]).astype(jnp.float32))
        h = h + (gate * (p[i] @ w_ple_proj[i]).astype(jnp.float32)).astype(h.dtype)
    return rmsnorm(h, g_final), jnp.stack(ks), jnp.stack(vs)


def setup_inputs(seed: int = 0) -> dict:
    key = jax.random.key(seed)
    ks = jax.random.split(key, 20)
    f32 = jnp.float32
    n_pages = PAST_LEN // PAGE_SIZE
    n_used = DEC_BATCH * n_pages
    n_phys = n_used + max(1, n_used // 4)
    nrm = lambda k, shape, s: jax.random.normal(k, shape, f32) * s
    page_table = jax.random.permutation(ks[6], n_phys)[:n_used].reshape(DEC_BATCH, n_pages).astype(jnp.int32)
    return {
        'x_prompt': nrm(ks[0], (BATCH, SEQ, D_MODEL), 1.0),
        'x_sample': nrm(ks[1], (DEC_BATCH, DEC_SEQ, D_MODEL), 1.0),
        'p_prompt': nrm(ks[2], (DEPTH, BATCH, SEQ, PLE_DIM), 1.0),
        'p_sample': nrm(ks[3], (DEPTH, DEC_BATCH, DEC_SEQ, PLE_DIM), 1.0),
        'cache_k': nrm(ks[4], (DEPTH, n_phys, PAGE_SIZE, N_HEADS, HEAD_DIM), 1.0),
        'cache_v': nrm(ks[5], (DEPTH, n_phys, PAGE_SIZE, N_HEADS, HEAD_DIM), 1.0),
        'page_table': page_table,
        'g_mix': 1.0 + nrm(ks[7], (DEPTH, D_MODEL), 0.05),
        'w_qkv': nrm(ks[8], (DEPTH, D_MODEL, 3 * D_MODEL), D_MODEL ** -0.5),
        'w_o': nrm(ks[9], (DEPTH, D_MODEL, D_MODEL), D_MODEL ** -0.5),
        'diff_lambda': nrm(ks[10], (N_DIFF_LAYERS, 4, DIFF_HALF), 0.1),
        'diff_subln': 1.0 + nrm(ks[11], (N_DIFF_LAYERS, HEAD_DIM), 0.05),
        'g_ffn': 1.0 + nrm(ks[12], (DEPTH, D_MODEL), 0.05),
        'w_up': nrm(ks[13], (DEPTH, D_MODEL, D_FF), D_MODEL ** -0.5),
        'w_down': nrm(ks[14], (DEPTH, D_FF, D_MODEL), D_FF ** -0.5),
        'g_ple': 1.0 + nrm(ks[15], (DEPTH, D_MODEL), 0.05),
        'w_ple_gate': nrm(ks[16], (DEPTH, D_MODEL, D_MODEL), D_MODEL ** -0.5),
        'w_ple_proj': nrm(ks[17], (DEPTH, PLE_DIM, D_MODEL), PLE_DIM ** -0.5),
        'g_final': 1.0 + nrm(ks[18], (D_MODEL,), 0.05),
    }


def reference(x_prompt, x_sample, p_prompt, p_sample, cache_k, cache_v, page_table,
              g_mix, w_qkv, w_o, diff_lambda, diff_subln, g_ffn, w_up, w_down,
              g_ple, w_ple_gate, w_ple_proj, g_final):
    weights = (g_mix, w_qkv, w_o, diff_lambda, diff_subln, g_ffn, w_up, w_down,
               g_ple, w_ple_gate, w_ple_proj, g_final)
    y_prompt, new_k_prompt, new_v_prompt = run_group(x_prompt, p_prompt, weights, None, None, None)
    y_sample, new_k_sample, new_v_sample = run_group(x_sample, p_sample, weights, cache_k, cache_v, page_table)
    return (y_prompt, y_sample, new_k_prompt, new_v_prompt, new_k_sample, new_v_sample)
```

```python
import functools
import math

import jax
import jax.numpy as jnp
from jax import lax
from jax.experimental import pallas as pl
from jax.experimental.pallas import tpu as pltpu

D_MODEL = 2048
N_HEADS = 16
HEAD_DIM = D_MODEL // N_HEADS
HEAD_GROUP = 8
DIFF_HALF = HEAD_DIM // 2
PLE_DIM = 256
MOBA_BLOCK = 256
MOBA_TOPK = 3
PAGE_SIZE = 128
NORM_EPS = 1e-6
SUBLN_EPS = 1e-5

F32 = jnp.float32
BF16 = jnp.bfloat16
NEG_INF = float("-inf")
MIB = 1024 * 1024

_NT = (((1,), (1,)), ((), ()))


def _params(semantics, vmem_mib):
    return pltpu.CompilerParams(dimension_semantics=semantics,
                                vmem_limit_bytes=vmem_mib * MIB)


def _rms(x, g, eps):
    return x * lax.rsqrt(jnp.mean(x * x, axis=-1, keepdims=True) + eps) * g


def _slope(head_plus_one):
    return jnp.exp2(head_plus_one * (-8.0 / N_HEADS))


def _dot(x, w):
    return jnp.dot(x.astype(BF16), w.astype(BF16), preferred_element_type=F32)


def _rms_kernel(x_ref, g_ref, o_ref):
    o_ref[...] = _rms(x_ref[...], g_ref[...], NORM_EPS).astype(o_ref.dtype)


def rmsnorm(x, g, out_dtype):
    m, d = x.shape
    tm = min(m, 512)
    return pl.pallas_call(
        _rms_kernel,
        out_shape=jax.ShapeDtypeStruct((m, d), out_dtype),
        grid=(m // tm,),
        in_specs=[pl.BlockSpec((tm, d), lambda i: (i, 0)),
                  pl.BlockSpec((1, d), lambda i: (0, 0))],
        out_specs=pl.BlockSpec((tm, d), lambda i: (i, 0)),
        compiler_params=_params(("parallel",), 32),
        name="rmsnorm",
    )(x, g.reshape(1, d))


def _mm_kernel(x_ref, w_ref, o_ref, *, relu2, per_head):
    acc = _dot(x_ref[...], w_ref[...])
    if relu2:
        acc = jnp.square(jnp.maximum(acc, 0.0))
    if per_head:
        for hh in range(o_ref.shape[1]):
            o_ref[:, hh, :] = acc[:, hh * HEAD_DIM:(hh + 1) * HEAD_DIM].astype(o_ref.dtype)
    else:
        o_ref[...] = acc.astype(o_ref.dtype)


def matmul(x, w, out_dtype, *, col0=0, n=None, relu2=False, per_head=False):
    m, k = x.shape
    n = w.shape[1] if n is None else n
    tm = min(m, 1024)
    tn = 1024
    jb = col0 // tn
    if per_head:
        hb = tn // HEAD_DIM
        out_shape = jax.ShapeDtypeStruct((m, n // HEAD_DIM, HEAD_DIM), out_dtype)
        out_spec = pl.BlockSpec((tm, hb, HEAD_DIM), lambda j, i: (i, j, 0))
    else:
        out_shape = jax.ShapeDtypeStruct((m, n), out_dtype)
        out_spec = pl.BlockSpec((tm, tn), lambda j, i: (i, j))
    return pl.pallas_call(
        functools.partial(_mm_kernel, relu2=relu2, per_head=per_head),
        out_shape=out_shape,
        grid=(n // tn, m // tm),
        in_specs=[pl.BlockSpec((tm, k), lambda j, i: (i, 0)),
                  pl.BlockSpec((k, tn), lambda j, i: (0, jb + j))],
        out_specs=out_spec,
        compiler_params=_params(("parallel", "parallel"), 48),
        name="matmul",
    )(x, w)


def _mm_res_kernel(x_ref, w_ref, h_ref, g_ref, oh_ref, oxn_ref, *scratch, nk):
    part = _dot(x_ref[...], w_ref[...])

    def finish(acc):
        hn = h_ref[...] + acc
        oh_ref[...] = hn
        oxn_ref[...] = _rms(hn, g_ref[...], NORM_EPS).astype(oxn_ref.dtype)

    if nk == 1:
        finish(part)
    else:
        acc_ref, = scratch
        kk = pl.program_id(1)

        @pl.when(kk == 0)
        def _():
            acc_ref[...] = part

        @pl.when(kk > 0)
        def _():
            acc_ref[...] += part

        @pl.when(kk == nk - 1)
        def _():
            finish(acc_ref[...])


def matmul_residual_rms(x, w, h, g):
    m, k = x.shape
    n = w.shape[1]
    tm = min(m, 512)
    tk = min(k, 1024) if k > 2048 else k
    nk = k // tk
    return pl.pallas_call(
        functools.partial(_mm_res_kernel, nk=nk),
        out_shape=(jax.ShapeDtypeStruct((m, n), F32), jax.ShapeDtypeStruct((m, n), BF16)),
        grid=(m // tm, nk),
        in_specs=[pl.BlockSpec((tm, tk), lambda i, kk: (i, kk)),
                  pl.BlockSpec((tk, n), lambda i, kk: (kk, 0)),
                  pl.BlockSpec((tm, n), lambda i, kk: (i, 0)),
                  pl.BlockSpec((1, n), lambda i, kk: (0, 0))],
        out_specs=(pl.BlockSpec((tm, n), lambda i, kk: (i, 0)),
                   pl.BlockSpec((tm, n), lambda i, kk: (i, 0))),
        scratch_shapes=[] if nk == 1 else [pltpu.VMEM((tm, n), F32)],
        compiler_params=_params(("parallel", "arbitrary"), 56),
        name="matmul_residual_rms",
    )(x, w, h, g.reshape(1, n))


def _ple_kernel(xn_ref, wg_ref, p_ref, wp_ref, h_ref, g_ref, *out_refs, final):
    z = _dot(xn_ref[...], wg_ref[...])
    gate = 1.0 / (1.0 + jnp.exp(-z))
    proj = _dot(p_ref[...], wp_ref[...])
    hn = h_ref[...] + gate * proj
    y = _rms(hn, g_ref[...], NORM_EPS)
    if final:
        out_refs[0][...] = y
    else:
        out_refs[0][...] = hn
        out_refs[1][...] = y.astype(out_refs[1].dtype)


def ple_layer(xn, wg, p, wp, h, g, final):
    m, d = h.shape
    tm = min(m, 256)
    row = lambda i: (i, 0)
    whole = lambda i: (0, 0)
    if final:
        out_shape = jax.ShapeDtypeStruct((m, d), F32)
        out_specs = pl.BlockSpec((tm, d), row)
    else:
        out_shape = (jax.ShapeDtypeStruct((m, d), F32), jax.ShapeDtypeStruct((m, d), BF16))
        out_specs = (pl.BlockSpec((tm, d), row), pl.BlockSpec((tm, d), row))
    return pl.pallas_call(
        functools.partial(_ple_kernel, final=final),
        out_shape=out_shape,
        grid=(m // tm,),
        in_specs=[pl.BlockSpec((tm, d), row),
                  pl.BlockSpec((d, d), whole),
                  pl.BlockSpec((tm, PLE_DIM), row),
                  pl.BlockSpec((PLE_DIM, d), whole),
                  pl.BlockSpec((tm, d), row),
                  pl.BlockSpec((1, d), whole)],
        out_specs=out_specs,
        compiler_params=_params(("parallel",), 48),
        name="ple_layer",
    )(xn, wg, p, wp, h, g.reshape(1, d))


def _lambda_of(lam_ref, lam_init):
    lp = lam_ref[...]
    a = jnp.sum(lp[0:1, :] * lp[1:2, :], axis=-1, keepdims=True)
    b = jnp.sum(lp[2:3, :] * lp[3:4, :], axis=-1, keepdims=True)
    return jnp.exp(a) - jnp.exp(b) + lam_init


def _online_update(s, v16, m, l, acc):
    m_new = jnp.maximum(m, jnp.max(s, axis=-1, keepdims=True))
    alpha = jnp.exp(m - m_new)
    p = jnp.exp(s - m_new)
    l = alpha * l + jnp.sum(p, axis=-1, keepdims=True)
    acc = alpha * acc + jnp.dot(p.astype(BF16), v16, preferred_element_type=F32)
    return m_new, l, acc


def _diff_prompt_kernel(lam_ref, q_ref, k_ref, v_ref, g_ref, o_ref, k16_ref, v16_ref,
                        *, tq, lam_init):
    hh = pl.program_id(2)
    head = pl.program_id(1) * HEAD_GROUP + hh
    qi = pl.program_id(3)
    tk = tq

    @pl.when((hh == 0) & (qi == 0))
    def _():
        for g in range(HEAD_GROUP):
            k16_ref[g] = k_ref[:, g, :].astype(BF16)
            v16_ref[g] = v_ref[:, g, :].astype(BF16)

    slope = _slope(jnp.full((1, 1), head + 1, jnp.int32).astype(F32))
    q = q_ref[...] * (DIFF_HALF ** -0.5)
    lane = lax.broadcasted_iota(jnp.int32, q.shape, 1)
    qq = jnp.concatenate([jnp.where(lane < DIFF_HALF, q, 0.0),
                          jnp.where(lane >= DIFF_HALF, q, 0.0)], axis=0).astype(BF16)
    r = lax.broadcasted_iota(jnp.int32, (2 * tq, tk), 0)
    c = lax.broadcasted_iota(jnp.int32, (2 * tq, tk), 1)
    r = jnp.where(r >= tq, r - tq, r)
    rel = (c - r).astype(F32) * slope

    def scores(kb):
        ks = k16_ref[hh, pl.ds(pl.multiple_of(kb * tk, tk), tk), :]
        return lax.dot_general(qq, ks, _NT, preferred_element_type=F32)

    def values(kb):
        return v16_ref[hh, pl.ds(pl.multiple_of(kb * tk, tk), tk), :]

    m0 = jnp.full((2 * tq, 1), NEG_INF, F32)
    l0 = jnp.zeros((2 * tq, 1), F32)
    acc0 = jnp.zeros((2 * tq, HEAD_DIM), F32)
    s = jnp.where(c <= r, scores(qi) + rel, NEG_INF)
    carry = _online_update(s, values(qi), m0, l0, acc0)

    def body(kb, carry):
        off = ((kb - qi) * tk).astype(F32) * slope
        return _online_update(scores(kb) + rel + off, values(kb), *carry)

    _, l, acc = lax.fori_loop(0, qi, body, carry)
    out = acc / l
    lam = _lambda_of(lam_ref, lam_init)
    d = out[:tq] - lam * out[tq:]
    o_ref[...] = (_rms(d, g_ref[...], SUBLN_EPS) * (1.0 - lam_init)).astype(o_ref.dtype)


def _prompt_attention_specs(batch, seq, tq):
    nq = seq // tq
    q_spec = pl.BlockSpec((tq, HEAD_DIM), lambda b, g, h, i: (b * nq + i, g * HEAD_GROUP + h))
    kv_spec = pl.BlockSpec((seq, HEAD_GROUP, HEAD_DIM), lambda b, g, h, i: (b, g, 0))
    kv16 = pltpu.VMEM((HEAD_GROUP, seq, HEAD_DIM), BF16)
    return (batch, N_HEADS // HEAD_GROUP, HEAD_GROUP, nq), q_spec, kv_spec, kv16


def diff_attention_prompt(q, k, v, lam_p, subln_g, lam_init, batch, seq):
    tq = 256
    grid, q_spec, kv_spec, kv16 = _prompt_attention_specs(batch, seq, tq)
    return pl.pallas_call(
        functools.partial(_diff_prompt_kernel, tq=tq, lam_init=lam_init),
        out_shape=jax.ShapeDtypeStruct((batch * seq, D_MODEL), BF16),
        grid=grid,
        in_specs=[pl.BlockSpec((4, DIFF_HALF), lambda b, g, h, i: (0, 0)),
                  q_spec, kv_spec, kv_spec,
                  pl.BlockSpec((1, HEAD_DIM), lambda b, g, h, i: (0, 0))],
        out_specs=q_spec,
        scratch_shapes=[kv16, kv16],
        compiler_params=_params(("parallel", "parallel", "arbitrary", "arbitrary"), 48),
        name="diff_attention_prompt",
    )(lam_p, q, k, v, subln_g.reshape(1, HEAD_DIM))


def _moba_prompt_kernel(q_ref, k_ref, v_ref, o_ref, k16_ref, v16_ref, kmean_ref, *, nblk):
    hh = pl.program_id(2)
    head = pl.program_id(1) * HEAD_GROUP + hh
    qi = pl.program_id(3)
    tq = tk = MOBA_BLOCK

    @pl.when((hh == 0) & (qi == 0))
    def _():
        kmean_ref[...] = jnp.zeros_like(kmean_ref)
        for g in range(HEAD_GROUP):
            kg = k_ref[:, g, :]
            k16_ref[g] = kg.astype(BF16)
            v16_ref[g] = v_ref[:, g, :].astype(BF16)
            for n in range(nblk):
                kmean_ref[g, n:n + 1, :] = jnp.mean(kg[n * tk:(n + 1) * tk, :], axis=0, keepdims=True)

    slope = _slope(jnp.full((1, 1), head + 1, jnp.int32).astype(F32))
    q = q_ref[...]
    q16 = (q * (HEAD_DIM ** -0.5)).astype(BF16)
    gate = lax.dot_general(q.astype(BF16), kmean_ref[hh].astype(BF16), _NT,
                           preferred_element_type=F32)
    blk = lax.broadcasted_iota(jnp.int32, gate.shape, 1)
    gate = jnp.where(blk < qi, gate, NEG_INF)

    r = lax.broadcasted_iota(jnp.int32, (tq, tk), 0)
    c = lax.broadcasted_iota(jnp.int32, (tq, tk), 1)
    rel = (c - r).astype(F32) * slope

    def scores(kb):
        ks = k16_ref[hh, pl.ds(pl.multiple_of(kb * tk, tk), tk), :]
        return lax.dot_general(q16, ks, _NT, preferred_element_type=F32)

    def values(kb):
        return v16_ref[hh, pl.ds(pl.multiple_of(kb * tk, tk), tk), :]

    m0 = jnp.full((tq, 1), NEG_INF, F32)
    l0 = jnp.zeros((tq, 1), F32)
    acc0 = jnp.zeros((tq, HEAD_DIM), F32)
    s = jnp.where(c <= r, scores(qi) + rel, NEG_INF)
    carry = _online_update(s, values(qi), m0, l0, acc0)

    def body(kb, carry):
        g_kb = jnp.sum(jnp.where(blk == kb, gate, 0.0), axis=-1, keepdims=True)
        beats = (gate > g_kb) | ((gate == g_kb) & (blk < kb))
        rank = jnp.sum(beats.astype(F32), axis=-1, keepdims=True)
        off = ((kb - qi) * tk).astype(F32) * slope
        s = jnp.where(rank < MOBA_TOPK, scores(kb) + rel + off, NEG_INF)
        return _online_update(s, values(kb), *carry)

    _, l, acc = lax.fori_loop(0, qi, body, carry)
    o_ref[...] = (acc / l).astype(o_ref.dtype)


def moba_attention_prompt(q, k, v, batch, seq):
    tq = MOBA_BLOCK
    grid, q_spec, kv_spec, kv16 = _prompt_attention_specs(batch, seq, tq)
    return pl.pallas_call(
        functools.partial(_moba_prompt_kernel, nblk=seq // tq),
        out_shape=jax.ShapeDtypeStruct((batch * seq, D_MODEL), BF16),
        grid=grid,
        in_specs=[q_spec, kv_spec, kv_spec],
        out_specs=q_spec,
        scratch_shapes=[kv16, kv16, pltpu.VMEM((HEAD_GROUP, HEAD_DIM, HEAD_DIM), F32)],
        compiler_params=_params(("parallel", "parallel", "arbitrary", "arbitrary"), 48),
        name="moba_attention_prompt",
    )(q, k, v)


def _rb(x):
    return x.astype(BF16).astype(F32)


def _diff_sample_kernel(pt_ref, q_ref, kn_ref, vn_ref, lam_ref, g_ref, *rest,
                        pps, nsteps, past_len, lam_init):
    k_refs, v_refs = rest[:pps], rest[pps:2 * pps]
    o_ref, s_ref, m_ref, l_ref, acc_ref = rest[2 * pps:]
    step = pl.program_id(1)
    nrow = 2 * N_HEADS
    ncol = PAGE_SIZE * N_HEADS

    q = q_ref[...] * (DIFF_HALF ** -0.5)
    lane = lax.broadcasted_iota(jnp.int32, q.shape, 1)
    qall = jnp.concatenate([jnp.where(lane < DIFF_HALF, q, 0.0),
                            jnp.where(lane >= DIFF_HALF, q, 0.0)], axis=0).astype(BF16)
    kn = jnp.concatenate([kn_ref[...], kn_ref[...]], axis=0)
    s_new = jnp.sum(qall.astype(F32) * _rb(kn), axis=-1, keepdims=True)
    lam = _lambda_of(lam_ref, lam_init)

    @pl.when(step == 0)
    def _():
        m_ref[...] = s_new
        l_ref[...] = jnp.ones_like(l_ref)
        acc_ref[...] = jnp.zeros_like(acc_ref)

    @pl.when(step < nsteps)
    def _():
        row = lax.broadcasted_iota(jnp.int32, (nrow, ncol), 0)
        col = lax.broadcasted_iota(jnp.int32, (nrow, ncol), 1)
        same_head = (col % N_HEADS) == (row % N_HEADS)
        tok = (col // N_HEADS).astype(F32)
        hrow = lax.broadcasted_iota(jnp.int32, (nrow, 1), 0) % N_HEADS
        slope = _slope((hrow + 1).astype(F32))
        for u in range(pps):
            page = step * pps + u
            s = lax.dot_general(qall, k_refs[u][...].astype(BF16), _NT, preferred_element_type=F32)
            first = (past_len - page * PAGE_SIZE).astype(F32)
            s = jnp.where(same_head, s - slope * (first - tok), NEG_INF)
            s_ref[page] = s
            m_old = m_ref[...]
            m_new = jnp.maximum(m_old, jnp.max(s, axis=-1, keepdims=True))
            l_ref[...] = jnp.exp(m_old - m_new) * l_ref[...] + jnp.sum(jnp.exp(s - m_new), axis=-1, keepdims=True)
            m_ref[...] = m_new

    def mix(p):
        p = p / l_ref[...]
        return (p[:N_HEADS] - lam * p[N_HEADS:]).astype(BF16)

    @pl.when(step >= nsteps)
    def _():
        for u in range(pps):
            a = mix(jnp.exp(s_ref[(step - nsteps) * pps + u] - m_ref[...]))
            acc_ref[...] += jnp.dot(a, v_refs[u][...].astype(BF16), preferred_element_type=F32)

    @pl.when(step == 2 * nsteps - 1)
    def _():
        a_new = mix(jnp.exp(s_new - m_ref[...])).astype(F32)
        d = acc_ref[...] + a_new * _rb(vn_ref[...])
        o_ref[...] = _rms(d, g_ref[...], SUBLN_EPS) * (1.0 - lam_init)


def diff_attention_sample(q, k_new, v_new, cache_k, cache_v, page_table, lam_p, subln_g, lam_init, layer):
    b, n_pages = page_table.shape
    pps = 4
    nsteps = n_pages // pps
    nrow = 2 * N_HEADS
    ncol = PAGE_SIZE * N_HEADS
    k_spec = lambda u: pl.BlockSpec(
        (None, None, ncol, HEAD_DIM),
        lambda i, s, pt: (layer, pt[i, jnp.minimum(s, nsteps - 1) * pps + u], 0, 0))
    v_spec = lambda u: pl.BlockSpec(
        (None, None, ncol, HEAD_DIM),
        lambda i, s, pt: (layer, pt[i, jnp.maximum(s - nsteps, 0) * pps + u], 0, 0))
    head_spec = pl.BlockSpec((None, N_HEADS, HEAD_DIM), lambda i, s, pt: (i, 0, 0))
    grid_spec = pltpu.PrefetchScalarGridSpec(
        num_scalar_prefetch=1,
        grid=(b, 2 * nsteps),
        in_specs=[head_spec, head_spec, head_spec,
                  pl.BlockSpec((4, DIFF_HALF), lambda i, s, pt: (0, 0)),
                  pl.BlockSpec((1, HEAD_DIM), lambda i, s, pt: (0, 0))]
                 + [k_spec(u) for u in range(pps)] + [v_spec(u) for u in range(pps)],
        out_specs=head_spec,
        scratch_shapes=[pltpu.VMEM((n_pages, nrow, ncol), F32), pltpu.VMEM((nrow, 1), F32),
                        pltpu.VMEM((nrow, 1), F32), pltpu.VMEM((N_HEADS, HEAD_DIM), F32)],
    )
    return pl.pallas_call(
        functools.partial(_diff_sample_kernel, pps=pps, nsteps=nsteps, past_len=n_pages * PAGE_SIZE,
                          lam_init=lam_init),
        out_shape=jax.ShapeDtypeStruct((b, N_HEADS, HEAD_DIM), F32),
        grid_spec=grid_spec,
        compiler_params=_params(("parallel", "arbitrary"), 60),
        name="diff_attention_sample",
    )(page_table, q, k_new, v_new, lam_p, subln_g.reshape(1, HEAD_DIM),
      *([cache_k] * pps), *([cache_v] * pps))


def _block_mean_kernel(pt_ref, *refs, pps):
    k_refs, o_ref = refs[:pps], refs[pps]
    per_blk = MOBA_BLOCK // PAGE_SIZE
    for n in range(pps // per_blk):
        tot = jnp.sum(k_refs[per_blk * n][...], axis=0)
        for u in range(1, per_blk):
            tot = tot + jnp.sum(k_refs[per_blk * n + u][...], axis=0)
        tot = tot * (1.0 / MOBA_BLOCK)
        for h in range(N_HEADS):
            o_ref[n:n + 1, h * HEAD_DIM:(h + 1) * HEAD_DIM] = tot[h:h + 1, :]


def moba_block_means(cache_k, page_table, layer):
    b, n_pages = page_table.shape
    pps = 16
    per_blk = MOBA_BLOCK // PAGE_SIZE
    page_spec = lambda u: pl.BlockSpec(
        (None, None, PAGE_SIZE, N_HEADS, HEAD_DIM),
        lambda i, s, pt: (layer, pt[i, s * pps + u], 0, 0, 0))
    grid_spec = pltpu.PrefetchScalarGridSpec(
        num_scalar_prefetch=1,
        grid=(b, n_pages // pps),
        in_specs=[page_spec(u) for u in range(pps)],
        out_specs=pl.BlockSpec((None, pps // per_blk, D_MODEL), lambda i, s, pt: (i, s, 0)),
    )
    return pl.pallas_call(
        functools.partial(_block_mean_kernel, pps=pps),
        out_shape=jax.ShapeDtypeStruct((b, n_pages // per_blk, D_MODEL), F32),
        grid_spec=grid_spec,
        compiler_params=_params(("parallel", "arbitrary"), 48),
        name="moba_block_means",
    )(page_table, *([cache_k] * pps))


def _moba_select_kernel(q_ref, kmean_ref, o_ref, *, nblk):
    prod = _rb(kmean_ref[...]) * _rb(q_ref[...])
    lane = lax.broadcasted_iota(jnp.int32, (nblk, HEAD_DIM), 1)
    gate = jnp.zeros((nblk, HEAD_DIM), F32)
    for h in range(N_HEADS):
        g_h = jnp.sum(prod[:, h * HEAD_DIM:(h + 1) * HEAD_DIM], axis=-1, keepdims=True)
        gate = jnp.where(lane == h, g_h, gate)
    n_idx = lax.broadcasted_iota(jnp.int32, (nblk, HEAD_DIM), 0)
    rank = jnp.zeros((nblk, HEAD_DIM), jnp.int32)
    for other in range(nblk):
        g_o = gate[other:other + 1, :]
        beats = (g_o > gate) | ((g_o == gate) & (other < n_idx))
        rank = rank + beats.astype(jnp.int32)
    o_ref[...] = jnp.zeros_like(o_ref)
    for t in range(MOBA_TOPK):
        pick = jnp.sum(jnp.where(rank == t, n_idx.astype(F32), 0.0), axis=0, keepdims=True)
        o_ref[t:t + 1, :] = pick.astype(jnp.int32)


def moba_select(qkv, kmean):
    b, nblk, _ = kmean.shape
    return pl.pallas_call(
        functools.partial(_moba_select_kernel, nblk=nblk),
        out_shape=jax.ShapeDtypeStruct((b, 8, HEAD_DIM), jnp.int32),
        grid=(b,),
        in_specs=[pl.BlockSpec((None, 1, D_MODEL), lambda i: (i, 0, 0)),
                  pl.BlockSpec((None, nblk, D_MODEL), lambda i: (i, 0, 0))],
        out_specs=pl.BlockSpec((None, 8, HEAD_DIM), lambda i: (i, 0, 0)),
        compiler_params=_params(("parallel",), 32),
        name="moba_select",
    )(qkv, kmean)


def _moba_sample_kernel(pg_ref, pos_ref, q_ref, kn_ref, vn_ref, *rest, npg, past_len):
    k_refs, v_refs, o_ref = rest[:npg], rest[npg:2 * npg], rest[2 * npg]
    i = pl.program_id(0)
    head = pl.program_id(1)
    hh = head % HEAD_GROUP
    slope = _slope(jnp.full((1, 1), head + 1, jnp.int32).astype(F32))
    q = _rb(q_ref[...])
    scale = HEAD_DIM ** -0.5
    tok = lax.broadcasted_iota(jnp.int32, (PAGE_SIZE, 1), 0)
    s_new = jnp.sum(q * _rb(kn_ref[...]), axis=-1, keepdims=True) * scale
    scores = []
    m = s_new
    for u in range(npg):
        dist = past_len - (pos_ref[i, head * npg + u] + tok)
        s = jnp.sum(_rb(k_refs[u][:, hh, :]) * q, axis=-1, keepdims=True) * scale - slope * dist.astype(F32)
        scores.append(s)
        m = jnp.maximum(m, jnp.max(s, axis=0, keepdims=True))
    p_new = jnp.exp(s_new - m)
    ps = [jnp.exp(s - m) for s in scores]
    l = p_new
    for p in ps:
        l = l + jnp.sum(p, axis=0, keepdims=True)
    acc = _rb(p_new / l) * _rb(vn_ref[...])
    for u in range(npg):
        acc = acc + jnp.sum(_rb(ps[u] / l) * _rb(v_refs[u][:, hh, :]), axis=0, keepdims=True)
    o_ref[...] = acc


def moba_attention_sample(qkv, cache_k, cache_v, pages, positions, layer, past_len):
    b = qkv.shape[0]
    npg = pages.shape[1] // N_HEADS
    page_spec = lambda u: pl.BlockSpec(
        (None, None, PAGE_SIZE, HEAD_GROUP, HEAD_DIM),
        lambda i, h, pg, pos: (layer, pg[i, h * npg + u], 0, h // HEAD_GROUP, 0))
    vec_spec = lambda part: pl.BlockSpec(
        (None, 1, HEAD_DIM), lambda i, h, pg, pos: (i, 0, part * N_HEADS + h))
    grid_spec = pltpu.PrefetchScalarGridSpec(
        num_scalar_prefetch=2,
        grid=(b, N_HEADS),
        in_specs=[vec_spec(0), vec_spec(1), vec_spec(2)]
                 + [page_spec(u) for u in range(npg)] + [page_spec(u) for u in range(npg)],
        out_specs=pl.BlockSpec((None, 1, HEAD_DIM), lambda i, h, pg, pos: (i, 0, h)),
    )
    return pl.pallas_call(
        functools.partial(_moba_sample_kernel, npg=npg, past_len=past_len),
        out_shape=jax.ShapeDtypeStruct((b, 1, D_MODEL), F32),
        grid_spec=grid_spec,
        compiler_params=_params(("parallel", "arbitrary"), 32),
        name="moba_attention_sample",
    )(pages, positions, qkv, qkv, qkv, *([cache_k] * npg), *([cache_v] * npg))


def _lam_init(layer):
    return 0.8 - 0.6 * math.exp(-0.3 * layer)


def _dense_tail(layer, depth, o, h, p, w, final_g):
    h, xn = matmul_residual_rms(o, w["w_o"][layer], h, w["g_ffn"][layer])
    u = matmul(xn, w["w_up"][layer], BF16, relu2=True)
    h, xn = matmul_residual_rms(u, w["w_down"][layer], h, w["g_ple"][layer])
    if layer == depth - 1:
        return ple_layer(xn, w["w_ple_gate"][layer], p, w["w_ple_proj"][layer], h, final_g, True), None
    return ple_layer(xn, w["w_ple_gate"][layer], p, w["w_ple_proj"][layer], h,
                     w["g_mix"][layer + 1], False)


def _prompt_group(x, p, w, g_final):
    batch, seq, d = x.shape
    depth = p.shape[0]
    h = x.reshape(batch * seq, d)
    xn = rmsnorm(h, w["g_mix"][0], BF16)
    ks, vs = [], []
    for layer in range(depth):
        q = matmul(xn, w["w_qkv"][layer], F32, col0=0, n=d)
        k = matmul(xn, w["w_qkv"][layer], F32, col0=d, n=d, per_head=True)
        v = matmul(xn, w["w_qkv"][layer], F32, col0=2 * d, n=d, per_head=True)
        ks.append(k)
        vs.append(v)
        if layer % 2 == 0:
            o = diff_attention_prompt(q, k, v, w["diff_lambda"][layer // 2], w["diff_subln"][layer // 2],
                                      _lam_init(layer), batch, seq)
        else:
            o = moba_attention_prompt(q, k, v, batch, seq)
        h, xn = _dense_tail(layer, depth, o, h, p[layer].reshape(batch * seq, PLE_DIM), w, g_final)
    shape = (depth, batch, seq, N_HEADS, HEAD_DIM)
    return h.reshape(batch, seq, d), jnp.stack(ks).reshape(shape), jnp.stack(vs).reshape(shape)


def _sample_group(x, p, w, g_final, cache_k, cache_v, page_table):
    batch, _, d = x.shape
    depth = p.shape[0]
    n_pages = page_table.shape[1]
    past_len = n_pages * PAGE_SIZE
    per_blk = MOBA_BLOCK // PAGE_SIZE
    rows = lambda c: c.reshape(c.shape[0], c.shape[1], PAGE_SIZE * N_HEADS, HEAD_DIM)
    h = x.reshape(batch, d)
    xn = rmsnorm(h, w["g_mix"][0], BF16)
    ks, vs = [], []
    for layer in range(depth):
        qkv = matmul(xn, w["w_qkv"][layer], F32)
        heads = qkv.reshape(batch, 3, N_HEADS, HEAD_DIM)
        ks.append(heads[:, 1])
        vs.append(heads[:, 2])
        if layer % 2 == 0:
            o = diff_attention_sample(heads[:, 0], heads[:, 1], heads[:, 2], rows(cache_k), rows(cache_v),
                                      page_table, w["diff_lambda"][layer // 2],
                                      w["diff_subln"][layer // 2], _lam_init(layer), layer)
        else:
            qkv3 = qkv.reshape(batch, 1, 3 * d)
            kmean = moba_block_means(cache_k, page_table, layer)
            top = moba_select(qkv3, kmean)[:, :MOBA_TOPK, :N_HEADS]
            blocks = jnp.transpose(top, (0, 2, 1))[..., None] * per_blk + jnp.arange(per_blk)
            blocks = blocks.reshape(batch, N_HEADS * MOBA_TOPK * per_blk)
            pages = jnp.take_along_axis(page_table, blocks, axis=1)
            o = moba_attention_sample(qkv3, cache_k, cache_v, pages, blocks * PAGE_SIZE, layer, past_len)
        h, xn = _dense_tail(layer, depth, o.reshape(batch, d), h, p[layer].reshape(batch, PLE_DIM),
                            w, g_final)
    shape = (depth, batch, 1, N_HEADS, HEAD_DIM)
    return h.reshape(batch, 1, d), jnp.stack(ks).reshape(shape), jnp.stack(vs).reshape(shape)


def kernel(x_prompt, x_sample, p_prompt, p_sample, cache_k, cache_v, page_table, g_mix, w_qkv, w_o,
           diff_lambda, diff_subln, g_ffn, w_up, w_down, g_ple, w_ple_gate, w_ple_proj, g_final):
    w = dict(g_mix=g_mix, w_qkv=w_qkv.astype(BF16), w_o=w_o.astype(BF16), diff_lambda=diff_lambda,
             diff_subln=diff_subln, g_ffn=g_ffn, w_up=w_up.astype(BF16), w_down=w_down.astype(BF16),
             g_ple=g_ple, w_ple_gate=w_ple_gate.astype(BF16), w_ple_proj=w_ple_proj.astype(BF16))
    y_p, k_p, v_p = _prompt_group(x_prompt, p_prompt, w, g_final)
    y_s, k_s, v_s = _sample_group(x_sample, p_sample, w, g_final, cache_k, cache_v, page_table)
    return (y_p, y_s, k_p, v_p, k_s, v_s)
```
